```python
import jax
import jax.numpy as jnp
from jax import lax
import numpy as np

D_MODEL = 2048
BATCH = 16
SEQ = 256
DEPTH = 1
DEC_BATCH = 8
DEC_SEQ = 4096
PAST_LEN = 256

GRID_W = 64
D_MIX = D_MODEL
D_CONV = D_MIX // 2
D_RNN = D_MIX - D_CONV
D_IN = 2 * D_CONV + 2 * D_RNN
CONV_W = 31
LRU_CONV_W = 4
LRU_BLOCKS = 16
LRU_BLK = D_RNN // LRU_BLOCKS
LRU_C = 8.0
N_KEYS = 128
N_EXPERTS = N_KEYS * N_KEYS
PEER_HEADS = 8
PEER_TOPK = 16
D_KEY = 128
TOKEN_BLOCK = 128
ALPHA = (2.0 * DEPTH) ** 0.25
BETA = (8.0 * DEPTH) ** -0.25

kernel_name = "hymba_conformer_rglru_peer_step"


def _layer_norm(x, g=None, b=None, eps=1e-6):
    xf = x.astype(jnp.float32)
    mu = jnp.mean(xf, axis=-1, keepdims=True)
    var = jnp.mean(jnp.square(xf - mu), axis=-1, keepdims=True)
    y = (xf - mu) * lax.rsqrt(var + eps)
    if g is not None:
        y = y * g.astype(jnp.float32) + b.astype(jnp.float32)
    return y.astype(x.dtype)


def _dwconv(x, w, b, pad):
    y = lax.conv_general_dilated(x, w[:, None, :], window_strides=(1,), padding=[pad],
                                 dimension_numbers=('NWC', 'WIO', 'NWC'),
                                 feature_group_count=x.shape[-1])
    return y + b


def _grid_pos_emb(rows, cols):
    quarter = D_MODEL // 4
    omega = 1.0 / (10000.0 ** (jnp.arange(quarter, dtype=jnp.float32) / quarter))
    pr = jnp.arange(rows, dtype=jnp.float32)[:, None] * omega
    pc = jnp.arange(cols, dtype=jnp.float32)[:, None] * omega
    er = jnp.concatenate([jnp.sin(pr), jnp.cos(pr)], axis=-1)
    ec = jnp.concatenate([jnp.sin(pc), jnp.cos(pc)], axis=-1)
    emb = jnp.concatenate([jnp.broadcast_to(er[:, None, :], (rows, cols, D_MODEL // 2)),
                           jnp.broadcast_to(ec[None, :, :], (rows, cols, D_MODEL // 2))], axis=-1)
    return emb.reshape(rows * cols, D_MODEL)


def _linear_scan(a, b, h0, reverse):
    if reverse:
        b = b.at[:, -1].add(a[:, -1] * h0)
    else:
        b = b.at[:, 0].add(a[:, 0] * h0)

    def combine(e1, e2):
        a1, b1 = e1
        a2, b2 = e2
        return a1 * a2, a2 * b1 + b2

    _, h = lax.associative_scan(combine, (a, b), axis=1, reverse=reverse)
    return h


def _rglru(xr, h0, w_a, b_a, w_x, b_x, lam, reverse):
    bsz, t = xr.shape[0], xr.shape[1]
    xb = xr.reshape(bsz, t, LRU_BLOCKS, LRU_BLK)
    r = jax.nn.sigmoid(jnp.einsum('btnk,nkj->btnj', xb, w_a).reshape(bsz, t, D_RNN) + b_a)
    i = jax.nn.sigmoid(jnp.einsum('btnk,nkj->btnj', xb, w_x).reshape(bsz, t, D_RNN) + b_x)
    log_a = -LRU_C * r.astype(jnp.float32) * jax.nn.softplus(-lam.astype(jnp.float32))
    a = jnp.exp(log_a)
    bterm = jnp.sqrt(-jnp.expm1(2.0 * log_a)) * (i * xr).astype(jnp.float32)
    return _linear_scan(a, bterm, h0.astype(jnp.float32), reverse)


def _mixer(u, h0, w_in, conv_w, conv_b, conv_ln_g, conv_ln_b, lru_conv_w, lru_conv_b,
           lru_wa, lru_ba, lru_wx, lru_bx, lru_lam, w_out):
    p = u @ w_in
    c_val, c_gate, r_gate, r_x = jnp.split(p, [D_CONV, 2 * D_CONV, 2 * D_CONV + D_RNN], axis=-1)
    z = c_val * jax.nn.sigmoid(c_gate)
    z = _dwconv(z, conv_w, conv_b, (CONV_W // 2, CONV_W // 2))
    z = jax.nn.silu(_layer_norm(z, conv_ln_g, conv_ln_b))
    xr = _dwconv(r_x, lru_conv_w, lru_conv_b, (2, 1))
    h_f = _rglru(xr, h0[:, 0], lru_wa[0], lru_ba[0], lru_wx[0], lru_bx[0], lru_lam[0], False)
    h_b = _rglru(xr, h0[:, 1], lru_wa[1], lru_ba[1], lru_wx[1], lru_bx[1], lru_lam[1], True)
    y_r = (h_f + h_b).astype(u.dtype) * jax.nn.gelu(r_gate)
    y = jnp.concatenate([z, y_r], axis=-1) @ w_out
    h_final = jnp.stack([h_f[:, -1], h_b[:, 0]], axis=1).astype(u.dtype)
    return y, h_final


def _peer(u, w_query, sub_keys, peer_u, peer_v):
    bsz, t, d = u.shape
    blocks = u.reshape(-1, TOKEN_BLOCK, d)

    def one_block(xb):
        q = (xb @ w_query).reshape(TOKEN_BLOCK, PEER_HEADS, 2, D_KEY)
        s = jnp.einsum('chpd,hpnd->chpn', q, sub_keys).astype(jnp.float32)
        sv, si = lax.top_k(s, PEER_TOPK)
        cand = (sv[:, :, 0, :, None] + sv[:, :, 1, None, :]).reshape(TOKEN_BLOCK, PEER_HEADS, PEER_TOPK * PEER_TOPK)
        cand_id = (si[:, :, 0, :, None] * N_KEYS + si[:, :, 1, None, :]).reshape(TOKEN_BLOCK, PEER_HEADS, PEER_TOPK * PEER_TOPK)
        top_s, top_pos = lax.top_k(cand, PEER_TOPK)
        ids = jnp.take_along_axis(cand_id, top_pos, axis=-1)
        g = jax.nn.softmax(top_s, axis=-1).astype(xb.dtype)
        ue = jnp.take(peer_u, ids, axis=0)
        act = jax.nn.gelu(jnp.einsum('chkd,cd->chk', ue, xb))
        ve = jnp.take(peer_v, ids, axis=0)
        return jnp.einsum('chk,chkd->cd', g * act, ve)

    return lax.map(one_block, blocks).reshape(bsz, t, d)


def _layer(x, mod, h0, w_in, conv_w, conv_b, conv_ln_g, conv_ln_b, lru_conv_w, lru_conv_b,
           lru_wa, lru_ba, lru_wx, lru_bx, lru_lam, w_out, ln1_g, ln1_b,
           w_query, sub_keys, peer_u, peer_v, ln2_g, ln2_b):
    sh1, sc1, g1, sh2, sc2, g2 = jnp.split(mod[:, None, :], 6, axis=-1)
    u = _layer_norm(x) * (1.0 + sc1) + sh1
    y, h_final = _mixer(u, h0, w_in, conv_w, conv_b, conv_ln_g, conv_ln_b, lru_conv_w, lru_conv_b,
                        lru_wa, lru_ba, lru_wx, lru_bx, lru_lam, w_out)
    x = _layer_norm(ALPHA * x + g1 * y, ln1_g, ln1_b)
    u = _layer_norm(x) * (1.0 + sc2) + sh2
    x = _layer_norm(ALPHA * x + g2 * _peer(u, w_query, sub_keys, peer_u, peer_v), ln2_g, ln2_b)
    return x, h_final


def setup_inputs(seed: int = 0) -> dict:
    key = jax.random.key(seed)
    ks = jax.random.split(key, 32)
    f32 = jnp.float32
    n = lambda k, shape, s: jax.random.normal(k, shape, f32) * s
    p_a = jax.random.uniform(ks[16], (DEPTH, 2, D_RNN), f32, 0.9, 0.999) ** (1.0 / LRU_C)
    return {
        'x_prompt': n(ks[0], (BATCH, SEQ, D_MODEL), 1.0),
        'x_sample': n(ks[1], (DEC_BATCH, DEC_SEQ, D_MODEL), 1.0),
        'state_rglru': n(ks[2], (DEC_BATCH, DEPTH, 2, D_RNN), 0.5),
        'c': n(ks[3], (DEC_BATCH, D_MODEL), 1.0),
        'c_ctx': n(ks[4], (D_MODEL,), 1.0),
        'w_ada': n(ks[5], (DEPTH, D_MODEL, 6 * D_MODEL), D_MODEL ** -0.5),
        'b_ada': n(ks[6], (DEPTH, 6 * D_MODEL), 0.01),
        'w_in': n(ks[7], (DEPTH, D_MODEL, D_IN), D_MODEL ** -0.5),
        'conv_w': n(ks[8], (DEPTH, CONV_W, D_CONV), CONV_W ** -0.5),
        'conv_b': n(ks[9], (DEPTH, D_CONV), 0.01),
        'conv_ln_g': 1.0 + n(ks[10], (DEPTH, D_CONV), 0.01),
        'conv_ln_b': n(ks[11], (DEPTH, D_CONV), 0.01),
        'lru_conv_w': n(ks[12], (DEPTH, LRU_CONV_W, D_RNN), LRU_CONV_W ** -0.5),
        'lru_conv_b': n(ks[13], (DEPTH, D_RNN), 0.01),
        'lru_wa': n(ks[14], (DEPTH, 2, LRU_BLOCKS, LRU_BLK, LRU_BLK), LRU_BLK ** -0.5),
        'lru_ba': n(ks[15], (DEPTH, 2, D_RNN), 0.01),
        'lru_wx': n(ks[17], (DEPTH, 2, LRU_BLOCKS, LRU_BLK, LRU_BLK), LRU_BLK ** -0.5),
        'lru_bx': n(ks[18], (DEPTH, 2, D_RNN), 0.01),
        'lru_lam': jnp.log(p_a) - jnp.log1p(-p_a),
        'w_out': n(ks[19], (DEPTH, D_MIX, D_MODEL), BETA * D_MIX ** -0.5),
        'ln1_g': 1.0 + n(ks[20], (DEPTH, D_MODEL), 0.01),
        'ln1_b': n(ks[21], (DEPTH, D_MODEL), 0.01),
        'w_query': n(ks[22], (DEPTH, D_MODEL, PEER_HEADS * 2 * D_KEY), D_MODEL ** -0.5),
        'sub_keys': n(ks[23], (DEPTH, PEER_HEADS, 2, N_KEYS, D_KEY), D_KEY ** -0.5),
        'peer_u': n(ks[24], (DEPTH, N_EXPERTS, D_MODEL), D_MODEL ** -0.5),
        'peer_v': n(ks[25], (DEPTH, N_EXPERTS, D_MODEL), BETA * PEER_HEADS ** -0.5),
        'ln2_g': 1.0 + n(ks[26], (DEPTH, D_MODEL), 0.01),
        'ln2_b': n(ks[27], (DEPTH, D_MODEL), 0.01),
    }


def reference(x_prompt, x_sample, state_rglru, c, c_ctx, w_ada, b_ada, w_in, conv_w, conv_b,
              conv_ln_g, conv_ln_b, lru_conv_w, lru_conv_b, lru_wa, lru_ba, lru_wx, lru_bx,
              lru_lam, w_out, ln1_g, ln1_b, w_query, sub_keys, peer_u, peer_v, ln2_g, ln2_b):
    x_p = x_prompt
    h_zero = jnp.zeros((x_prompt.shape[0], 2, D_RNN), x_prompt.dtype)
    rows = x_sample.shape[1] // GRID_W
    x_s = x_sample + _grid_pos_emb(rows, GRID_W).astype(x_sample.dtype)
    finals = []
    for l in range(DEPTH):
        lw = (w_in[l], conv_w[l], conv_b[l], conv_ln_g[l], conv_ln_b[l], lru_conv_w[l], lru_conv_b[l],
              lru_wa[l], lru_ba[l], lru_wx[l], lru_bx[l], lru_lam[l], w_out[l], ln1_g[l], ln1_b[l],
              w_query[l], sub_keys[l], peer_u[l], peer_v[l], ln2_g[l], ln2_b[l])
        mod_ctx = (jax.nn.silu(c_ctx) @ w_ada[l] + b_ada[l])[None, :]
        mod_lat = jax.nn.silu(c) @ w_ada[l] + b_ada[l]
        x_p, h_fin = _layer(x_p, mod_ctx, h_zero, *lw)
        finals.append(h_fin)
        x_s, _ = _layer(x_s, mod_lat, state_rglru[:, l], *lw)
    new_state_rglru = jnp.stack(finals, axis=1)
    return (x_p, x_s, new_state_rglru)
```

```python
import functools
import math

import jax
import jax.numpy as jnp
from jax import lax
from jax.experimental import pallas as pl
from jax.experimental.pallas import tpu as pltpu

F32 = jnp.float32
BF16 = jnp.bfloat16

GRID_W = 64
LRU_C = 8.0
PEER_TOPK = 16
LN_EPS = 1e-6
HALO = 16
SUBLANES = 8
LANES = 128
MXU_DIM = 256
VMEM_LIMIT = 56 * 1024 * 1024


def _cparams(sem):
    return pltpu.CompilerParams(dimension_semantics=sem, vmem_limit_bytes=VMEM_LIMIT)


def _resident(shape, index_map):
    return pl.BlockSpec(shape, index_map, pipeline_mode=pl.Buffered(1))


def _ln(x):
    mu = jnp.mean(x, axis=-1, keepdims=True)
    xc = x - mu
    var = jnp.mean(xc * xc, axis=-1, keepdims=True)
    return xc * lax.rsqrt(var + LN_EPS)


def _gelu(x):
    return 0.5 * x * (1.0 + jnp.tanh(0.7978845608028654 * (x + 0.044715 * (x * x * x))))


def _sigmoid(x):
    return 1.0 / (1.0 + jnp.exp(-x))


def _mod_kernel(c_ref, w_ref, b_ref, o_ref):
    cs = c_ref[...]
    s = cs * _sigmoid(cs)
    o_ref[...] = jnp.dot(s.astype(BF16), w_ref[...].astype(BF16),
                         preferred_element_type=F32) + b_ref[...]


def _modulation(cvecs, w_ada, b_ada):
    rows, d = cvecs.shape
    n = w_ada.shape[1]
    bn = 1024 if n % 1024 == 0 else n
    return pl.pallas_call(
        _mod_kernel,
        out_shape=jax.ShapeDtypeStruct((rows, n), F32),
        grid=(n // bn,),
        in_specs=[pl.BlockSpec((rows, d), lambda j: (0, 0)),
                  pl.BlockSpec((d, bn), lambda j: (0, j)),
                  pl.BlockSpec((1, bn), lambda j: (0, j))],
        out_specs=pl.BlockSpec((rows, bn), lambda j: (0, j)),
        compiler_params=_cparams(("arbitrary",)),
        name="adaln_mod",
    )(cvecs, w_ada, b_ada.reshape(1, n))


def _inproj_kernel(*refs, has_pos):
    if has_pos:
        x_ref, pos_ref, mod_ref, w_ref, p_ref = refs
        x = x_ref[0] + pos_ref[...]
    else:
        x_ref, mod_ref, w_ref, p_ref = refs
        x = x_ref[0]
    u = _ln(x) * (1.0 + mod_ref[0, 1:2, :]) + mod_ref[0, 0:1, :]
    p_ref[0] = jnp.dot(u.astype(BF16), w_ref[...], preferred_element_type=F32)


def _inproj(x, pos, mod3, mod_row, w_in16, tt):
    b, t, d = x.shape
    n = w_in16.shape[1]
    has_pos = pos is not None
    in_specs = [pl.BlockSpec((1, tt, d), lambda i, j: (i, j, 0))]
    args = [x]
    if has_pos:
        in_specs.append(pl.BlockSpec((tt, d), lambda i, j: (j, 0)))
        args.append(pos)
    in_specs += [pl.BlockSpec((1, 6, d), lambda i, j: (mod_row(i), 0, 0)),
                 _resident((d, n), lambda i, j: (0, 0))]
    args += [mod3, w_in16]
    return pl.pallas_call(
        functools.partial(_inproj_kernel, has_pos=has_pos),
        out_shape=jax.ShapeDtypeStruct((b, t, n), F32),
        grid=(b, t // tt),
        in_specs=in_specs,
        out_specs=pl.BlockSpec((1, tt, n), lambda i, j: (i, j, 0)),
        compiler_params=_cparams(("arbitrary", "arbitrary")),
        name="ln_inproj",
    )(*args)


def _scan_group(a, b, rows, reverse):
    for dist in (1, 2, 4):
        if reverse:
            keep = rows < SUBLANES - dist
            shift = SUBLANES - dist
        else:
            keep = rows >= dist
            shift = dist
        a_n = jnp.where(keep, pltpu.roll(a, shift, 0), 1.0)
        b_n = jnp.where(keep, pltpu.roll(b, shift, 0), 0.0)
        b = a * b_n + b
        a = a * a_n
    return a, b


def _mixer_fwd_kernel(cv_ref, cg_ref, rx_ref, cvl_ref, cgl_ref, rxl_ref, cvr_ref, cgr_ref, rxr_ref,
                      cw_ref, cb_ref, lg_ref, lb_ref, lw_ref, lbias_ref, gw_ref, gb_ref, lam_ref,
                      h0_ref,
                      za_ref, hf_ref, ab_ref, bb_ref, hfin_ref,
                      zs, zph, rs, rph, xr_s, a_s, b_s, carry,
                      *, tt, cw, n_t, kw, lkw, sb, rc):
    t = pl.program_id(1)
    lm = jnp.where(t > 0, 1.0, 0.0)
    rm = jnp.where(t < n_t - 1, 1.0, 0.0)
    pad = kw // 2

    zs[0:HALO, :] = cvl_ref[0] * _sigmoid(cgl_ref[0]) * lm
    zs[HALO:HALO + tt, :] = cv_ref[0] * _sigmoid(cg_ref[0])
    zs[HALO + tt:HALO + tt + HALO, :] = cvr_ref[0] * _sigmoid(cgr_ref[0]) * rm
    rs[0:HALO, :] = rxl_ref[0] * lm
    rs[HALO:HALO + tt, :] = rx_ref[0]
    rs[HALO + tt:HALO + tt + HALO, :] = rxr_ref[0] * rm

    ext = tt + 3 * SUBLANES
    for r in range(SUBLANES):
        zph[r] = zs[r:r + ext, :]
    for k in range(lkw):
        rph[k] = rs[HALO - 2 + k:HALO - 2 + k + tt, :]

    def conv_chunk(ci, _):
        r0 = pl.multiple_of(ci * rc, rc)
        acc = jnp.broadcast_to(cb_ref[...], (rc, cw))
        for k in range(kw):
            q, r = divmod(k + HALO - pad, SUBLANES)
            acc = acc + cw_ref[k:k + 1, :] * zph[r, pl.ds(pl.multiple_of(r0 + SUBLANES * q, SUBLANES), rc), :]
        y = _ln(acc) * lg_ref[...] + lb_ref[...]
        y = y * _sigmoid(y)
        za_ref[0, pl.ds(r0, rc), :] = y.astype(BF16)
        xr = jnp.broadcast_to(lbias_ref[...], (rc, cw))
        for k in range(lkw):
            xr = xr + lw_ref[k:k + 1, :] * rph[k, pl.ds(r0, rc), :]
        xr_s[pl.ds(r0, rc), :] = xr
        return 0

    lax.fori_loop(0, tt // rc, conv_chunk, 0)

    lam = lam_ref[...]
    neg = -lam
    sp = jnp.maximum(neg, 0.0) + jnp.log(1.0 + jnp.exp(-jnp.abs(neg)))
    for s in range(cw // sb):
        cs = slice(s * sb, (s + 1) * sb)
        xsb = xr_s[:, cs]
        x16 = xsb.astype(BF16)
        for d in range(2):
            r = _sigmoid(jnp.dot(x16, gw_ref[2 * d, s], preferred_element_type=F32)
                         + gb_ref[2 * d:2 * d + 1, cs])
            i = _sigmoid(jnp.dot(x16, gw_ref[2 * d + 1, s], preferred_element_type=F32)
                         + gb_ref[2 * d + 1:2 * d + 2, cs])
            log_a = (-LRU_C) * r * sp[d:d + 1, cs]
            a = jnp.exp(log_a)
            bt = jnp.sqrt(1.0 - jnp.exp(2.0 * log_a)) * (i * xsb)
            if d == 0:
                a_s[:, cs] = a
                b_s[:, cs] = bt
            else:
                ab_ref[0, :, cs] = a
                bb_ref[0, :, cs] = bt

    @pl.when(t == 0)
    def _():
        carry[...] = jnp.broadcast_to(h0_ref[0, 0:1, :], (SUBLANES, cw))

    rows = lax.broadcasted_iota(jnp.int32, (SUBLANES, cw), 0)

    def scan_grp(g, h_in):
        r0 = pl.multiple_of(g * SUBLANES, SUBLANES)
        a, b = _scan_group(a_s[pl.ds(r0, SUBLANES), :], b_s[pl.ds(r0, SUBLANES), :], rows, False)
        h = a * h_in + b
        hf_ref[0, pl.ds(r0, SUBLANES), :] = h
        return jnp.broadcast_to(h[SUBLANES - 1:SUBLANES, :], (SUBLANES, cw))

    h_last = lax.fori_loop(0, tt // SUBLANES, scan_grp, carry[...], unroll=2)
    carry[...] = h_last

    @pl.when(t == n_t - 1)
    def _():
        hfin_ref[0] = h_last[0:1, :]


def _mixer_fwd(p, h0, prm, tt):
    b, t, _ = p.shape
    cw = prm["conv_w"].shape[1]
    kw = prm["conv_w"].shape[0]
    lkw = prm["lru_conv_w"].shape[0]
    n_t = t // tt
    hb = tt // HALO
    n_hb = t // HALO
    sb = prm["gate_w"].shape[-1]
    rc = 32

    def main(col):
        return pl.BlockSpec((1, tt, cw), lambda i, j: (i, j, col))

    def left(col):
        return pl.BlockSpec((1, HALO, cw), lambda i, j: (i, jnp.maximum(j * hb - 1, 0), col))

    def right(col):
        return pl.BlockSpec((1, HALO, cw), lambda i, j: (i, jnp.minimum((j + 1) * hb, n_hb - 1), col))

    def full(a):
        nd = a.ndim
        return pl.BlockSpec(a.shape, lambda i, j: (0,) * nd)

    small = [prm["conv_w"], prm["conv_b"], prm["conv_ln_g"], prm["conv_ln_b"], prm["lru_conv_w"],
             prm["lru_conv_b"], prm["gate_w"], prm["gate_b"], prm["lam"]]
    in_specs = ([main(0), main(1), main(3), left(0), left(1), left(3), right(0), right(1), right(3)]
                + [full(a) for a in small]
                + [pl.BlockSpec((1, 2, cw), lambda i, j: (i, 0, 0))])
    seq = jax.ShapeDtypeStruct((b, t, cw), F32)
    out_shape = (jax.ShapeDtypeStruct((b, t, cw), BF16), seq, seq, seq,
                 jax.ShapeDtypeStruct((b, 1, cw), F32))
    blk = pl.BlockSpec((1, tt, cw), lambda i, j: (i, j, 0))
    out_specs = (blk, blk, blk, blk, pl.BlockSpec((1, 1, cw), lambda i, j: (i, 0, 0)))
    scratch = [pltpu.VMEM((tt + 2 * HALO, cw), F32),
               pltpu.VMEM((SUBLANES, tt + 3 * SUBLANES, cw), F32),
               pltpu.VMEM((tt + 2 * HALO, cw), F32),
               pltpu.VMEM((lkw, tt, cw), F32),
               pltpu.VMEM((tt, cw), F32),
               pltpu.VMEM((tt, cw), F32),
               pltpu.VMEM((tt, cw), F32),
               pltpu.VMEM((SUBLANES, cw), F32)]
    return pl.pallas_call(
        functools.partial(_mixer_fwd_kernel, tt=tt, cw=cw, n_t=n_t, kw=kw, lkw=lkw, sb=sb, rc=rc),
        out_shape=out_shape,
        grid=(b, n_t),
        in_specs=in_specs,
        out_specs=out_specs,
        scratch_shapes=scratch,
        compiler_params=_cparams(("arbitrary", "arbitrary")),
        name="mixer_fwd",
    )(p, p, p, p, p, p, p, p, p, *small, h0)


def _mixer_bwd_kernel(*refs, has_pos, tt, cw, n_t, alpha):
    if has_pos:
        (x_ref, pos_ref, mod_ref, rg_ref, za_ref, hf_ref, ab_ref, bb_ref, h0_ref, wo_ref, g_ref, b_ref,
         x1_ref, u2t_ref, hfin_ref, hb_s, carry) = refs
        x = x_ref[0] + pos_ref[...]
    else:
        (x_ref, mod_ref, rg_ref, za_ref, hf_ref, ab_ref, bb_ref, h0_ref, wo_ref, g_ref, b_ref,
         x1_ref, u2t_ref, hfin_ref, hb_s, carry) = refs
        x = x_ref[0]
    t = pl.program_id(1)

    @pl.when(t == 0)
    def _():
        carry[...] = jnp.broadcast_to(h0_ref[0, 1:2, :], (SUBLANES, cw))

    rows = lax.broadcasted_iota(jnp.int32, (SUBLANES, cw), 0)
    n_g = tt // SUBLANES

    def scan_grp(g, h_in):
        r0 = pl.multiple_of((n_g - 1 - g) * SUBLANES, SUBLANES)
        a, b = _scan_group(ab_ref[0, pl.ds(r0, SUBLANES), :], bb_ref[0, pl.ds(r0, SUBLANES), :], rows, True)
        h = a * h_in + b
        hb_s[pl.ds(r0, SUBLANES), :] = h
        return jnp.broadcast_to(h[0:1, :], (SUBLANES, cw))

    h_first = lax.fori_loop(0, n_g, scan_grp, carry[...], unroll=2)
    carry[...] = h_first

    @pl.when(t == n_t - 1)
    def _():
        hfin_ref[0] = h_first[0:1, :]

    y_r = (hf_ref[0] + hb_s[...]) * _gelu(rg_ref[0])
    y = (jnp.dot(za_ref[0], wo_ref[0:cw, :], preferred_element_type=F32)
         + jnp.dot(y_r.astype(BF16), wo_ref[cw:2 * cw, :], preferred_element_type=F32))
    x1 = _ln(alpha * x + mod_ref[0, 2:3, :] * y) * g_ref[...] + b_ref[...]
    x1_ref[0] = x1
    u2 = _ln(x1) * (1.0 + mod_ref[0, 4:5, :]) + mod_ref[0, 3:4, :]
    u2t_ref[...] = u2.T.astype(BF16)


def _mixer_bwd(x, pos, mod3, mod_row, p, za, hf, ab, bb, h0, w_out16, ln_g, ln_b, tt, alpha):
    b, t, d = x.shape
    cw = za.shape[-1]
    n_t = t // tt
    has_pos = pos is not None

    def rev(j):
        return n_t - 1 - j

    in_specs = [pl.BlockSpec((1, tt, d), lambda i, j: (i, rev(j), 0))]
    args = [x]
    if has_pos:
        in_specs.append(pl.BlockSpec((tt, d), lambda i, j: (rev(j), 0)))
        args.append(pos)
    seq = pl.BlockSpec((1, tt, cw), lambda i, j: (i, rev(j), 0))
    in_specs += [pl.BlockSpec((1, 6, d), lambda i, j: (mod_row(i), 0, 0)),
                 pl.BlockSpec((1, tt, cw), lambda i, j: (i, rev(j), 2)),
                 seq, seq, seq, seq,
                 pl.BlockSpec((1, 2, cw), lambda i, j: (i, 0, 0)),
                 _resident((2 * cw, d), lambda i, j: (0, 0)),
                 pl.BlockSpec((1, d), lambda i, j: (0, 0)),
                 pl.BlockSpec((1, d), lambda i, j: (0, 0))]
    args += [mod3, p, za, hf, ab, bb, h0, w_out16, ln_g, ln_b]
    out_shape = (jax.ShapeDtypeStruct((b, t, d), F32),
                 jax.ShapeDtypeStruct((d, b * t), BF16),
                 jax.ShapeDtypeStruct((b, 1, cw), F32))
    out_specs = (pl.BlockSpec((1, tt, d), lambda i, j: (i, rev(j), 0)),
                 pl.BlockSpec((d, tt), lambda i, j: (0, i * n_t + rev(j))),
                 pl.BlockSpec((1, 1, cw), lambda i, j: (i, 0, 0)))
    return pl.pallas_call(
        functools.partial(_mixer_bwd_kernel, has_pos=has_pos, tt=tt, cw=cw, n_t=n_t, alpha=alpha),
        out_shape=out_shape,
        grid=(b, n_t),
        in_specs=in_specs,
        out_specs=out_specs,
        scratch_shapes=[pltpu.VMEM((tt, cw), F32), pltpu.VMEM((SUBLANES, cw), F32)],
        compiler_params=_cparams(("arbitrary", "arbitrary")),
        name="mixer_bwd_outproj",
    )(*args)


def _candidate_pairs(k):
    return [(a, b) for a in range(k) for b in range(k) if (a + 1) * (b + 1) <= k]


def _top_rows(s, rowid, k):
    rank = jnp.full(s.shape, float(k), F32)
    work = s
    vals = []
    for it in range(k):
        m = jnp.max(work, axis=0, keepdims=True)
        first = jnp.min(jnp.where(work == m, rowid, float(s.shape[0])), axis=0, keepdims=True)
        hit = rowid == first
        rank = jnp.where(hit, float(it), rank)
        work = jnp.where(hit, -jnp.inf, work)
        vals.append(m)
    return rank, vals


def _route_kernel(u2t_ref, wq_ref, keys_ref, rank1_ref, lrow_ref, ea_ref, eb_ref, q_s, *, heads, nk, topk):
    tb = u2t_ref.shape[1]
    q_s[...] = jnp.dot(wq_ref[...], u2t_ref[...], preferred_element_type=F32).astype(BF16)
    pairs = _candidate_pairs(topk)
    n_c = len(pairs)
    n_cp = -(-n_c // SUBLANES) * SUBLANES
    rowid = lax.broadcasted_iota(jnp.int32, (nk, tb), 0).astype(F32)
    crow = lax.broadcasted_iota(jnp.int32, (n_cp, tb), 0).astype(F32)

    def head(h, _):
        q0 = q_s[pl.ds(pl.multiple_of(h * 2 * nk, nk), nk), :]
        q1 = q_s[pl.ds(pl.multiple_of(h * 2 * nk + nk, nk), nk), :]
        s0 = jnp.dot(keys_ref[2 * h], q0, preferred_element_type=F32)
        s1 = jnp.dot(keys_ref[2 * h + 1], q1, preferred_element_type=F32)
        rank0, v0 = _top_rows(s0, rowid, topk)
        rank1, v1 = _top_rows(s1, rowid, topk)
        rows_c = [v0[a] + v1[b] for a, b in pairs]
        rows_c += [jnp.full((1, tb), -jnp.inf, F32)] * (n_cp - n_c)
        cand = jnp.concatenate(rows_c, axis=0)
        m0 = v0[0] + v1[0]
        sel = jnp.zeros((n_cp, tb), F32)
        z = jnp.zeros((1, tb), F32)
        for _it in range(topk):
            m = jnp.max(cand, axis=0, keepdims=True)
            first = jnp.min(jnp.where(cand == m, crow, float(n_cp)), axis=0, keepdims=True)
            hit = crow == first
            sel = jnp.where(hit, 1.0, sel)
            cand = jnp.where(hit, -jnp.inf, cand)
            z = z + jnp.exp(m - m0)
        lrow = jnp.zeros((nk, tb), F32)
        for a in range(topk):
            idx = [c for c, (pa, _pb) in enumerate(pairs) if pa == a]
            la = sel[idx[0]:idx[0] + 1, :]
            for c in idx[1:]:
                la = la + sel[c:c + 1, :]
            lrow = jnp.where(rank0 == float(a), la, lrow)
        rank1_ref[h] = rank1
        lrow_ref[h] = lrow
        ea_ref[h] = jnp.exp(s0 - v0[0])
        eb_ref[h] = jnp.exp(s1 - v1[0]) / z
        return 0

    lax.fori_loop(0, heads, head, 0)


def _route(u2t, wq_t16, keys16, heads, nk, tb):
    d, n = u2t.shape
    qd = wq_t16.shape[0]
    tab = jax.ShapeDtypeStruct((heads, nk, n), F32)
    blk = pl.BlockSpec((heads, nk, tb), lambda i: (0, 0, i))
    return pl.pallas_call(
        functools.partial(_route_kernel, heads=heads, nk=nk, topk=PEER_TOPK),
        out_shape=(tab, tab, tab, tab),
        grid=(n // tb,),
        in_specs=[pl.BlockSpec((d, tb), lambda i: (0, i)),
                  _resident((qd, d), lambda i: (0, 0)),
                  _resident(keys16.shape, lambda i: (0, 0, 0))],
        out_specs=(blk, blk, blk, blk),
        scratch_shapes=[pltpu.VMEM((qd, tb), BF16)],
        compiler_params=_cparams(("arbitrary",)),
        name="peer_route",
    )(u2t, wq_t16, keys16)


def _peer_kernel(u2t_ref, u_ref, vt_ref, rank1_ref, lrow_ref, ea_ref, eb_ref, x1_ref, mod_ref, g_ref, b_ref,
                 o_ref, acc, st, wt, *, heads, nk, alpha):
    e = pl.program_id(1)
    n_e = pl.num_programs(1)
    eb_rows, tb = st.shape
    per = eb_rows // nk

    st[...] = jnp.dot(u_ref[...], u2t_ref[...], preferred_element_type=F32)

    for il in range(per):
        rs = slice(il * nk, (il + 1) * nk)
        for lc in range(tb // LANES):
            ls = slice(lc * LANES, (lc + 1) * LANES)
            g = jnp.zeros((nk, LANES), F32)
            for h in range(heads):
                lrow = lrow_ref[h, il:il + 1, ls]
                ea = ea_ref[h, il:il + 1, ls]
                g = g + jnp.where(rank1_ref[h, :, ls] < lrow, eb_ref[h, :, ls], 0.0) * ea
            wt[rs, ls] = (g * _gelu(st[rs, ls])).astype(BF16)

    contrib = jnp.dot(vt_ref[...], wt[...], preferred_element_type=F32)

    @pl.when(e == 0)
    def _():
        acc[...] = contrib

    @pl.when(e > 0)
    def _():
        acc[...] += contrib

    @pl.when(e == n_e - 1)
    def _():
        y = acc[...].T
        v = alpha * x1_ref[...] + mod_ref[0, 5:6, :] * y
        o_ref[...] = _ln(v) * g_ref[...] + b_ref[...]


def _peer(u2t, u16, vt16, tabs, x1, mod3, mod_row, ln_g, ln_b, heads, nk, tb, eb, alpha):
    d, n = u2t.shape
    n_exp = u16.shape[0]
    tab = _resident((heads, nk, tb), lambda i, j: (0, 0, i))
    key_rows = pl.BlockSpec((heads, eb // nk, tb), lambda i, j: (0, j, i))
    return pl.pallas_call(
        functools.partial(_peer_kernel, heads=heads, nk=nk, alpha=alpha),
        out_shape=jax.ShapeDtypeStruct((n, d), F32),
        grid=(n // tb, n_exp // eb),
        in_specs=[_resident((d, tb), lambda i, j: (0, i)),
                  pl.BlockSpec((eb, d), lambda i, j: (j, 0)),
                  pl.BlockSpec((d, eb), lambda i, j: (0, j)),
                  tab, key_rows, key_rows, tab,
                  _resident((tb, d), lambda i, j: (i, 0)),
                  pl.BlockSpec((1, 6, d), lambda i, j: (mod_row(i), 0, 0)),
                  pl.BlockSpec((1, d), lambda i, j: (0, 0)),
                  pl.BlockSpec((1, d), lambda i, j: (0, 0))],
        out_specs=pl.BlockSpec((tb, d), lambda i, j: (i, 0)),
        scratch_shapes=[pltpu.VMEM((d, tb), F32), pltpu.VMEM((eb, tb), F32), pltpu.VMEM((eb, tb), BF16)],
        compiler_params=_cparams(("arbitrary", "arbitrary")),
        name="peer_dense",
    )(u2t, u16, vt16, *tabs, x1, mod3, ln_g, ln_b)


def _grid_pos_table(rows, cols, d):
    quarter = d // 4
    omega = 1.0 / (10000.0 ** (jnp.arange(quarter, dtype=F32) / quarter))
    pr = jnp.arange(rows, dtype=F32)[:, None] * omega
    pc = jnp.arange(cols, dtype=F32)[:, None] * omega
    er = jnp.concatenate([jnp.sin(pr), jnp.cos(pr)], axis=-1)
    ec = jnp.concatenate([jnp.sin(pc), jnp.cos(pc)], axis=-1)
    emb = jnp.concatenate([jnp.broadcast_to(er[:, None, :], (rows, cols, d // 2)),
                           jnp.broadcast_to(ec[None, :, :], (rows, cols, d // 2))], axis=-1)
    return emb.reshape(rows * cols, d)


def _pack_block_diag(w, width):
    nblk, k, _ = w.shape
    per = width // k
    wr = w.reshape(nblk // per, per, k, k)
    eye = jnp.eye(per, dtype=w.dtype)
    return jnp.einsum("spkj,pq->spkqj", wr, eye).reshape(nblk // per, width, width)


def _tile(t, pref):
    return pref if t % pref == 0 else t


def kernel(x_prompt, x_sample, state_rglru, c, c_ctx, w_ada, b_ada, w_in, conv_w, conv_b, conv_ln_g, conv_ln_b, lru_conv_w, lru_conv_b, lru_wa, lru_ba, lru_wx, lru_bx, lru_lam, w_out, ln1_g, ln1_b, w_query, sub_keys, peer_u, peer_v, ln2_g, ln2_b):
    depth, d, _ = w_in.shape
    bp, tp, _ = x_prompt.shape
    bs, ts, _ = x_sample.shape
    cw = conv_w.shape[-1]
    heads, _, nk, dk = sub_keys.shape[1:]
    alpha = (2.0 * depth) ** 0.25
    assert lru_conv_w.shape[-1] == cw and w_in.shape[-1] == 4 * cw and nk == dk == LANES

    pos = _grid_pos_table(ts // GRID_W, GRID_W, d)
    n_ctx = 1
    rows = -(-(n_ctx + bs) // SUBLANES) * SUBLANES
    cvecs = jnp.zeros((rows, d), F32).at[0].set(c_ctx).at[n_ctx:n_ctx + bs].set(c)
    sb = min(MXU_DIM, cw)

    x_p, x_s = x_prompt, x_sample
    finals = []
    for l in range(depth):
        mod3 = _modulation(cvecs, w_ada[l], b_ada[l]).reshape(rows, 6, d)
        w_in16 = w_in[l].astype(BF16)
        w_out16 = w_out[l].astype(BF16)
        wq_t16 = w_query[l].T.astype(BF16)
        keys16 = sub_keys[l].reshape(heads * 2, nk, dk).astype(BF16)
        u16 = peer_u[l].astype(BF16)
        vt16 = peer_v[l].T.astype(BF16)
        prm = {
            "conv_w": conv_w[l], "conv_b": conv_b[l][None], "conv_ln_g": conv_ln_g[l][None],
            "conv_ln_b": conv_ln_b[l][None], "lru_conv_w": lru_conv_w[l], "lru_conv_b": lru_conv_b[l][None],
            "gate_w": jnp.stack([_pack_block_diag(lru_wa[l, 0], sb), _pack_block_diag(lru_wx[l, 0], sb),
                                 _pack_block_diag(lru_wa[l, 1], sb), _pack_block_diag(lru_wx[l, 1], sb)]
                                ).astype(BF16),
            "gate_b": jnp.stack([lru_ba[l, 0], lru_bx[l, 0], lru_ba[l, 1], lru_bx[l, 1]]),
            "lam": lru_lam[l],
        }

        def run_group(x, pos_tab, mod_row_seq, mod_row_tok, h0, tb):
            b, t, _ = x.shape
            tt = _tile(t, 256)
            p = _inproj(x, pos_tab, mod3, mod_row_seq, w_in16, tt)
            za, hf, ab, bb, hfin_f = _mixer_fwd(p, h0, prm, tt)
            x1, u2t, hfin_b = _mixer_bwd(x, pos_tab, mod3, mod_row_seq, p, za, hf, ab, bb, h0, w_out16,
                                         ln1_g[l][None], ln1_b[l][None], tt, alpha)
            tabs = _route(u2t, wq_t16, keys16, heads, nk, _tile(b * t, 256))
            x2 = _peer(u2t, u16, vt16, tabs, x1.reshape(b * t, d), mod3, mod_row_tok(tb),
                       ln2_g[l][None], ln2_b[l][None], heads, nk, tb, _tile(nk * nk, SUBLANES * nk), alpha)
            return x2.reshape(b, t, d), jnp.concatenate([hfin_f, hfin_b], axis=1)

        tb_p = _tile(bp * tp, 512)
        tb_s = _tile(ts, 512)
        x_p, h_fin = run_group(x_p, None, lambda i: 0, lambda tb: (lambda i: 0),
                               jnp.zeros((bp, 2, cw), F32), tb_p)
        finals.append(h_fin)
        x_s, _ = run_group(x_s, pos if l == 0 else None, lambda i: n_ctx + i,
                           lambda tb: (lambda i: n_ctx + (i * tb) // ts),
                           state_rglru[:, l], tb_s)
    return (x_p, x_s, jnp.stack(finals, axis=1))
```

```python
import functools

import jax
import jax.numpy as jnp
from jax import lax
from jax.experimental import pallas as pl
from jax.experimental.pallas import tpu as pltpu

F32 = jnp.float32
BF16 = jnp.bfloat16

GRID_W = 64
LRU_C = 8.0
PEER_TOPK = 16
LN_EPS = 1e-6
HALO = 16
SUBLANES = 8
LANES = 128
PACK = 2 * SUBLANES
MXU_DIM = 256
VMEM_LIMIT = 56 * 1024 * 1024


def _cparams(sem, flags=None):
    return pltpu.CompilerParams(dimension_semantics=sem, vmem_limit_bytes=VMEM_LIMIT, flags=flags)


def _resident(shape, index_map):
    return pl.BlockSpec(shape, index_map, pipeline_mode=pl.Buffered(1))


def _ln(x):
    mu = jnp.mean(x, axis=-1, keepdims=True)
    xc = x - mu
    var = jnp.mean(xc * xc, axis=-1, keepdims=True)
    return xc * lax.rsqrt(var + LN_EPS)


def _gelu(x):
    return 0.5 * x * (1.0 + jnp.tanh(0.7978845608028654 * (x + 0.044715 * (x * x * x))))


def _sigmoid(x):
    return 1.0 / (1.0 + jnp.exp(-x))


def _mod_kernel(c_ref, w_ref, b_ref, o_ref):
    cs = c_ref[...]
    s = cs * _sigmoid(cs)
    o_ref[...] = jnp.dot(s.astype(BF16), w_ref[...].astype(BF16),
                         preferred_element_type=F32) + b_ref[...]


def _modulation(cvecs, w_ada, b_ada):
    rows, d = cvecs.shape
    n = w_ada.shape[1]
    bn = 1024 if n % 1024 == 0 else n
    return pl.pallas_call(
        _mod_kernel,
        out_shape=jax.ShapeDtypeStruct((rows, n), F32),
        grid=(n // bn,),
        in_specs=[pl.BlockSpec((rows, d), lambda j: (0, 0)),
                  pl.BlockSpec((d, bn), lambda j: (0, j)),
                  pl.BlockSpec((1, bn), lambda j: (0, j))],
        out_specs=pl.BlockSpec((rows, bn), lambda j: (0, j)),
        compiler_params=_cparams(("arbitrary",)),
        name="adaln_mod",
    )(cvecs, w_ada, b_ada.reshape(1, n))


def _inproj_kernel(*refs, has_pos):
    if has_pos:
        x_ref, pos_ref, mod_ref, w_ref, p_ref = refs
        x = x_ref[0] + pos_ref[...]
    else:
        x_ref, mod_ref, w_ref, p_ref = refs
        x = x_ref[0]
    u = _ln(x) * (1.0 + mod_ref[0, 1:2, :]) + mod_ref[0, 0:1, :]
    p_ref[0] = jnp.dot(u.astype(BF16), w_ref[...], preferred_element_type=F32)


def _inproj(x, pos, mod3, mod_row, w_in16, tt):
    b, t, d = x.shape
    n = w_in16.shape[1]
    has_pos = pos is not None
    in_specs = [pl.BlockSpec((1, tt, d), lambda i, j: (i, j, 0))]
    args = [x]
    if has_pos:
        in_specs.append(pl.BlockSpec((tt, d), lambda i, j: (j, 0)))
        args.append(pos)
    in_specs += [pl.BlockSpec((1, 6, d), lambda i, j: (mod_row(i), 0, 0)),
                 _resident((d, n), lambda i, j: (0, 0))]
    args += [mod3, w_in16]
    return pl.pallas_call(
        functools.partial(_inproj_kernel, has_pos=has_pos),
        out_shape=jax.ShapeDtypeStruct((b, t, n), F32),
        grid=(b, t // tt),
        in_specs=in_specs,
        out_specs=pl.BlockSpec((1, tt, n), lambda i, j: (i, j, 0)),
        compiler_params=_cparams(("arbitrary", "arbitrary")),
        name="ln_inproj",
    )(*args)


def _scan_group(a, b, rows, reverse):
    for dist in (1, 2, 4):
        if reverse:
            keep = rows < SUBLANES - dist
            shift = SUBLANES - dist
        else:
            keep = rows >= dist
            shift = dist
        a_n = jnp.where(keep, pltpu.roll(a, shift, 0), 1.0)
        b_n = jnp.where(keep, pltpu.roll(b, shift, 0), 0.0)
        b = a * b_n + b
        a = a * a_n
    return a, b


def _mixer_fwd_kernel(cv_ref, cg_ref, rx_ref, cvl_ref, cgl_ref, rxl_ref, cvr_ref, cgr_ref, rxr_ref,
                      cw_ref, cb_ref, lg_ref, lb_ref, lw_ref, lbias_ref, gw_ref, gb_ref, lam_ref,
                      h0_ref,
                      za_ref, hf_ref, ab_ref, bb_ref, hfin_ref,
                      zs, zph, rs, rph, xr_s, a_s, b_s, carry,
                      *, tt, cw, n_t, kw, lkw, sb, rc):
    t = pl.program_id(1)
    lm = jnp.where(t > 0, 1.0, 0.0)
    rm = jnp.where(t < n_t - 1, 1.0, 0.0)
    pad = kw // 2

    zs[0:HALO, :] = cvl_ref[0] * _sigmoid(cgl_ref[0]) * lm
    zs[HALO:HALO + tt, :] = cv_ref[0] * _sigmoid(cg_ref[0])
    zs[HALO + tt:HALO + tt + HALO, :] = cvr_ref[0] * _sigmoid(cgr_ref[0]) * rm
    rs[0:HALO, :] = rxl_ref[0] * lm
    rs[HALO:HALO + tt, :] = rx_ref[0]
    rs[HALO + tt:HALO + tt + HALO, :] = rxr_ref[0] * rm

    ext = tt + 3 * SUBLANES
    for r in range(SUBLANES):
        zph[r] = zs[r:r + ext, :]
    for k in range(lkw):
        rph[k] = rs[HALO - 2 + k:HALO - 2 + k + tt, :]

    def conv_chunk(ci, _):
        r0 = pl.multiple_of(ci * rc, rc)
        acc = jnp.broadcast_to(cb_ref[...], (rc, cw))
        for k in range(kw):
            q, r = divmod(k + HALO - pad, SUBLANES)
            acc = acc + cw_ref[k:k + 1, :] * zph[r, pl.ds(pl.multiple_of(r0 + SUBLANES * q, SUBLANES), rc), :]
        y = _ln(acc) * lg_ref[...] + lb_ref[...]
        y = y * _sigmoid(y)
        za_ref[0, pl.ds(r0, rc), :] = y.astype(BF16)
        xr = jnp.broadcast_to(lbias_ref[...], (rc, cw))
        for k in range(lkw):
            xr = xr + lw_ref[k:k + 1, :] * rph[k, pl.ds(r0, rc), :]
        xr_s[pl.ds(r0, rc), :] = xr
        return 0

    lax.fori_loop(0, tt // rc, conv_chunk, 0)

    lam = lam_ref[...]
    neg = -lam
    sp = jnp.maximum(neg, 0.0) + jnp.log(1.0 + jnp.exp(-jnp.abs(neg)))
    for s in range(cw // sb):
        cs = slice(s * sb, (s + 1) * sb)
        xsb = xr_s[:, cs]
        x16 = xsb.astype(BF16)
        for d in range(2):
            r = _sigmoid(jnp.dot(x16, gw_ref[2 * d, s], preferred_element_type=F32)
                         + gb_ref[2 * d:2 * d + 1, cs])
            i = _sigmoid(jnp.dot(x16, gw_ref[2 * d + 1, s], preferred_element_type=F32)
                         + gb_ref[2 * d + 1:2 * d + 2, cs])
            log_a = (-LRU_C) * r * sp[d:d + 1, cs]
            a = jnp.exp(log_a)
            bt = jnp.sqrt(1.0 - jnp.exp(2.0 * log_a)) * (i * xsb)
            if d == 0:
                a_s[:, cs] = a
                b_s[:, cs] = bt
            else:
                ab_ref[0, :, cs] = a
                bb_ref[0, :, cs] = bt

    @pl.when(t == 0)
    def _():
        carry[...] = jnp.broadcast_to(h0_ref[0, 0:1, :], (SUBLANES, cw))

    rows = lax.broadcasted_iota(jnp.int32, (SUBLANES, cw), 0)

    def scan_grp(g, h_in):
        r0 = pl.multiple_of(g * SUBLANES, SUBLANES)
        a, b = _scan_group(a_s[pl.ds(r0, SUBLANES), :], b_s[pl.ds(r0, SUBLANES), :], rows, False)
        h = a * h_in + b
        hf_ref[0, pl.ds(r0, SUBLANES), :] = h
        return jnp.broadcast_to(h[SUBLANES - 1:SUBLANES, :], (SUBLANES, cw))

    h_last = lax.fori_loop(0, tt // SUBLANES, scan_grp, carry[...], unroll=2)
    carry[...] = h_last

    @pl.when(t == n_t - 1)
    def _():
        hfin_ref[0] = h_last[0:1, :]


def _mixer_fwd(p, h0, prm, tt):
    b, t, _ = p.shape
    cw = prm["conv_w"].shape[1]
    kw = prm["conv_w"].shape[0]
    lkw = prm["lru_conv_w"].shape[0]
    n_t = t // tt
    hb = tt // HALO
    n_hb = t // HALO
    sb = prm["gate_w"].shape[-1]
    rc = 32

    def main(col):
        return pl.BlockSpec((1, tt, cw), lambda i, j: (i, j, col))

    def left(col):
        return pl.BlockSpec((1, HALO, cw), lambda i, j: (i, jnp.maximum(j * hb - 1, 0), col))

    def right(col):
        return pl.BlockSpec((1, HALO, cw), lambda i, j: (i, jnp.minimum((j + 1) * hb, n_hb - 1), col))

    def full(a):
        nd = a.ndim
        return pl.BlockSpec(a.shape, lambda i, j: (0,) * nd)

    small = [prm["conv_w"], prm["conv_b"], prm["conv_ln_g"], prm["conv_ln_b"], prm["lru_conv_w"],
             prm["lru_conv_b"], prm["gate_w"], prm["gate_b"], prm["lam"]]
    in_specs = ([main(0), main(1), main(3), left(0), left(1), left(3), right(0), right(1), right(3)]
                + [full(a) for a in small]
                + [pl.BlockSpec((1, 2, cw), lambda i, j: (i, 0, 0))])
    seq = jax.ShapeDtypeStruct((b, t, cw), F32)
    out_shape = (jax.ShapeDtypeStruct((b, t, cw), BF16), seq, seq, seq,
                 jax.ShapeDtypeStruct((b, 1, cw), F32))
    blk = pl.BlockSpec((1, tt, cw), lambda i, j: (i, j, 0))
    out_specs = (blk, blk, blk, blk, pl.BlockSpec((1, 1, cw), lambda i, j: (i, 0, 0)))
    scratch = [pltpu.VMEM((tt + 2 * HALO, cw), F32),
               pltpu.VMEM((SUBLANES, tt + 3 * SUBLANES, cw), F32),
               pltpu.VMEM((tt + 2 * HALO, cw), F32),
               pltpu.VMEM((lkw, tt, cw), F32),
               pltpu.VMEM((tt, cw), F32),
               pltpu.VMEM((tt, cw), F32),
               pltpu.VMEM((tt, cw), F32),
               pltpu.VMEM((SUBLANES, cw), F32)]
    return pl.pallas_call(
        functools.partial(_mixer_fwd_kernel, tt=tt, cw=cw, n_t=n_t, kw=kw, lkw=lkw, sb=sb, rc=rc),
        out_shape=out_shape,
        grid=(b, n_t),
        in_specs=in_specs,
        out_specs=out_specs,
        scratch_shapes=scratch,
        compiler_params=_cparams(("arbitrary", "arbitrary")),
        name="mixer_fwd",
    )(p, p, p, p, p, p, p, p, p, *small, h0)


def _mixer_bwd_kernel(*refs, has_pos, tt, cw, n_t, alpha):
    if has_pos:
        (x_ref, pos_ref, mod_ref, rg_ref, za_ref, hf_ref, ab_ref, bb_ref, h0_ref, wo_ref, g_ref, b_ref,
         x1_ref, u2t_ref, hfin_ref, hb_s, carry) = refs
        x = x_ref[0] + pos_ref[...]
    else:
        (x_ref, mod_ref, rg_ref, za_ref, hf_ref, ab_ref, bb_ref, h0_ref, wo_ref, g_ref, b_ref,
         x1_ref, u2t_ref, hfin_ref, hb_s, carry) = refs
        x = x_ref[0]
    t = pl.program_id(1)

    @pl.when(t == 0)
    def _():
        carry[...] = jnp.broadcast_to(h0_ref[0, 1:2, :], (SUBLANES, cw))

    rows = lax.broadcasted_iota(jnp.int32, (SUBLANES, cw), 0)
    n_g = tt // SUBLANES

    def scan_grp(g, h_in):
        r0 = pl.multiple_of((n_g - 1 - g) * SUBLANES, SUBLANES)
        a, b = _scan_group(ab_ref[0, pl.ds(r0, SUBLANES), :], bb_ref[0, pl.ds(r0, SUBLANES), :], rows, True)
        h = a * h_in + b
        hb_s[pl.ds(r0, SUBLANES), :] = h
        return jnp.broadcast_to(h[0:1, :], (SUBLANES, cw))

    h_first = lax.fori_loop(0, n_g, scan_grp, carry[...], unroll=2)
    carry[...] = h_first

    @pl.when(t == n_t - 1)
    def _():
        hfin_ref[0] = h_first[0:1, :]

    y_r = (hf_ref[0] + hb_s[...]) * _gelu(rg_ref[0])
    y = (jnp.dot(za_ref[0], wo_ref[0:cw, :], preferred_element_type=F32)
         + jnp.dot(y_r.astype(BF16), wo_ref[cw:2 * cw, :], preferred_element_type=F32))
    x1 = _ln(alpha * x + mod_ref[0, 2:3, :] * y) * g_ref[...] + b_ref[...]
    x1_ref[0] = x1
    u2 = _ln(x1) * (1.0 + mod_ref[0, 4:5, :]) + mod_ref[0, 3:4, :]
    u2t_ref[...] = _packed_words(u2.T)


def _mixer_bwd(x, pos, mod3, mod_row, p, za, hf, ab, bb, h0, w_out16, ln_g, ln_b, tt, alpha):
    b, t, d = x.shape
    cw = za.shape[-1]
    n_t = t // tt
    has_pos = pos is not None

    def rev(j):
        return n_t - 1 - j

    in_specs = [pl.BlockSpec((1, tt, d), lambda i, j: (i, rev(j), 0))]
    args = [x]
    if has_pos:
        in_specs.append(pl.BlockSpec((tt, d), lambda i, j: (rev(j), 0)))
        args.append(pos)
    seq = pl.BlockSpec((1, tt, cw), lambda i, j: (i, rev(j), 0))
    in_specs += [pl.BlockSpec((1, 6, d), lambda i, j: (mod_row(i), 0, 0)),
                 pl.BlockSpec((1, tt, cw), lambda i, j: (i, rev(j), 2)),
                 seq, seq, seq, seq,
                 pl.BlockSpec((1, 2, cw), lambda i, j: (i, 0, 0)),
                 _resident((2 * cw, d), lambda i, j: (0, 0)),
                 pl.BlockSpec((1, d), lambda i, j: (0, 0)),
                 pl.BlockSpec((1, d), lambda i, j: (0, 0))]
    args += [mod3, p, za, hf, ab, bb, h0, w_out16, ln_g, ln_b]
    out_shape = (jax.ShapeDtypeStruct((b, t, d), F32),
                 jax.ShapeDtypeStruct((d // 2, b * t), jnp.int32),
                 jax.ShapeDtypeStruct((b, 1, cw), F32))
    out_specs = (pl.BlockSpec((1, tt, d), lambda i, j: (i, rev(j), 0)),
                 pl.BlockSpec((d // 2, tt), lambda i, j: (0, i * n_t + rev(j))),
                 pl.BlockSpec((1, 1, cw), lambda i, j: (i, 0, 0)))
    return pl.pallas_call(
        functools.partial(_mixer_bwd_kernel, has_pos=has_pos, tt=tt, cw=cw, n_t=n_t, alpha=alpha),
        out_shape=out_shape,
        grid=(b, n_t),
        in_specs=in_specs,
        out_specs=out_specs,
        scratch_shapes=[pltpu.VMEM((tt, cw), F32), pltpu.VMEM((SUBLANES, cw), F32)],
        compiler_params=_cparams(("arbitrary", "arbitrary")),
        name="mixer_bwd_outproj",
    )(*args)


def _candidate_pairs(k):
    return [(a, b) for a in range(k) for b in range(k) if (a + 1) * (b + 1) <= k]


def _take_max(work, rowid, exact):
    m = jnp.max(work, axis=0, keepdims=True)
    hit = work == m
    if exact:
        first = jnp.min(jnp.where(hit, rowid, float(work.shape[0])), axis=0, keepdims=True)
        hit = rowid == first
    return m, hit


def _top_rows(s, rowid, k, exact):
    rank = jnp.full(s.shape, float(k), F32)
    work = s
    vals = []
    for it in range(k):
        m, hit = _take_max(work, rowid, exact)
        rank = jnp.where(hit, float(it), rank)
        work = jnp.where(hit, -jnp.inf, work)
        vals.append(m)
    return rank, vals


def _pair_bf16(x):
    bits = lax.bitcast_convert_type(x.astype(BF16).astype(F32), jnp.int32)
    return bits | lax.shift_right_logical(bits, 16)


def _packed_words(x):
    return pltpu.bitcast(x.astype(BF16), jnp.int32)


def _route_kernel(u2t_ref, wq_ref, keys_ref, rank1_ref, lrow_ref, ea_ref, eb_ref, q_s, *, heads, nk, topk):
    tb = u2t_ref.shape[1]
    q_s[...] = jnp.dot(wq_ref[...], pltpu.bitcast(u2t_ref[...], BF16), preferred_element_type=F32).astype(BF16)
    pairs = _candidate_pairs(topk)
    n_c = len(pairs)
    n_cp = -(-n_c // SUBLANES) * SUBLANES
    rowid = lax.broadcasted_iota(jnp.int32, (nk, tb), 0).astype(F32)
    crow = lax.broadcasted_iota(jnp.int32, (n_cp, tb), 0).astype(F32)

    def head(h, _):
        q0 = q_s[pl.ds(pl.multiple_of(h * 2 * nk, nk), nk), :]
        q1 = q_s[pl.ds(pl.multiple_of(h * 2 * nk + nk, nk), nk), :]
        s0 = jnp.dot(keys_ref[2 * h], q0, preferred_element_type=F32)
        s1 = jnp.dot(keys_ref[2 * h + 1], q1, preferred_element_type=F32)

        def select(exact):
            rank0, v0 = _top_rows(s0, rowid, topk, exact)
            rank1, v1 = _top_rows(s1, rowid, topk, exact)
            rows_c = [v0[a] + v1[b] for a, b in pairs]
            rows_c += [jnp.full((1, tb), -jnp.inf, F32)] * (n_cp - n_c)
            cand = jnp.concatenate(rows_c, axis=0)
            m0 = v0[0] + v1[0]
            sel = jnp.zeros((n_cp, tb), F32)
            z = jnp.zeros((1, tb), F32)
            for _it in range(topk):
                m, hit = _take_max(cand, crow, exact)
                sel = jnp.where(hit, 1.0, sel)
                cand = jnp.where(hit, -jnp.inf, cand)
                z = z + jnp.exp(m - m0)
            return rank0, rank1, v0[0], v1[0], sel, z

        def emit(rank0, rank1, top0, top1, sel, z):
            lrow = jnp.zeros((nk, tb), F32)
            for a in range(topk):
                idx = [c for c, (pa, _pb) in enumerate(pairs) if pa == a]
                la = sel[idx[0]:idx[0] + 1, :]
                for c in idx[1:]:
                    la = la + sel[c:c + 1, :]
                lrow = jnp.where(rank0 == float(a), la, lrow)
            rank1_ref[h] = _packed_words(rank1)
            lrow_ref[h] = _pair_bf16(lrow)
            ea_ref[h] = _pair_bf16(jnp.exp(s0 - top0))
            eb_ref[h] = _packed_words(jnp.exp(s1 - top1) / z)

        quick = select(False)
        counts = (jnp.sum(jnp.where(quick[0] < float(topk), 1.0, 0.0), axis=0, keepdims=True),
                  jnp.sum(jnp.where(quick[1] < float(topk), 1.0, 0.0), axis=0, keepdims=True),
                  jnp.sum(quick[4], axis=0, keepdims=True))
        excess = sum(jnp.abs(cnt - float(topk)) for cnt in counts)
        tied = jnp.max(excess) > 0.0

        @pl.when(jnp.logical_not(tied))
        def _():
            emit(*quick)

        @pl.when(tied)
        def _():
            emit(*select(True))

        return 0

    lax.fori_loop(0, heads, head, 0)


def _route(u2t, wq_t16, keys16, heads, nk, tb):
    dh, n = u2t.shape
    qd, d = wq_t16.shape
    by_key = jax.ShapeDtypeStruct((heads, nk, n), jnp.int32)
    packed = jax.ShapeDtypeStruct((heads, nk // 2, n), jnp.int32)
    return pl.pallas_call(
        functools.partial(_route_kernel, heads=heads, nk=nk, topk=PEER_TOPK),
        out_shape=(packed, by_key, by_key, packed),
        grid=(n // tb,),
        in_specs=[pl.BlockSpec((dh, tb), lambda i: (0, i)),
                  _resident((qd, d), lambda i: (0, 0)),
                  _resident(keys16.shape, lambda i: (0, 0, 0))],
        out_specs=(pl.BlockSpec((heads, nk // 2, tb), lambda i: (0, 0, i)),
                   pl.BlockSpec((heads, nk, tb), lambda i: (0, 0, i)),
                   pl.BlockSpec((heads, nk, tb), lambda i: (0, 0, i)),
                   pl.BlockSpec((heads, nk // 2, tb), lambda i: (0, 0, i))),
        scratch_shapes=[pltpu.VMEM((qd, tb), BF16)],
        compiler_params=_cparams(("arbitrary",)),
        name="peer_route",
    )(u2t, wq_t16, keys16)


def _peer_kernel(u2t_ref, u_ref, vt_ref, rank1_ref, lrow_ref, ea_ref, eb_ref, x1_ref, mod_ref, g_ref, b_ref,
                 o_ref, acc, st, wt_a, wt_b, *, heads, nk, alpha, n_e):
    s = pl.program_id(0)
    e_lag = jnp.maximum(s - 1, 0) % n_e
    eb_rows, tb = st.shape
    d = acc.shape[0]
    n_sub = eb_rows // MXU_DIM
    d_rows = d // n_sub
    per = MXU_DIM // nk
    n_v = nk // PACK

    @pl.when(s == 0)
    def _():
        wt_b[...] = jnp.zeros(wt_b.shape, wt_b.dtype)

    @pl.when(e_lag == 0)
    def _():
        acc[...] = jnp.zeros(acc.shape, acc.dtype)

    def score(k):
        rs = slice(k * MXU_DIM, (k + 1) * MXU_DIM)
        u_rows = pltpu.bitcast(u_ref[k * MXU_DIM // 2:(k + 1) * MXU_DIM // 2, :], BF16)
        st[rs, :] = jnp.dot(u_rows, pltpu.bitcast(u2t_ref[...], BF16),
                            preferred_element_type=F32)

    def project(k, w_old):
        rs = slice(k * d_rows, (k + 1) * d_rows)
        vt_rows = pltpu.bitcast(vt_ref[k * d_rows // 2:(k + 1) * d_rows // 2, :], BF16)
        acc[rs, :] += jnp.dot(vt_rows, w_old[...], preferred_element_type=F32)

    def route(k, w_new):
        for il in range(k * per, (k + 1) * per):
            for lc in range(tb // LANES):
                ls = slice(lc * LANES, (lc + 1) * LANES)
                g = [None] * n_v
                for h in range(heads):
                    lrow = pltpu.bitcast(jnp.broadcast_to(lrow_ref[h, il:il + 1, ls], (SUBLANES, LANES)), BF16)
                    ea = pltpu.bitcast(jnp.broadcast_to(ea_ref[h, il:il + 1, ls], (SUBLANES, LANES)), BF16)
                    for v in range(n_v):
                        js = slice(v * SUBLANES, (v + 1) * SUBLANES)
                        hit = pltpu.bitcast(rank1_ref[h, js, ls], BF16) < lrow
                        term = jnp.where(hit, pltpu.bitcast(eb_ref[h, js, ls], BF16), jnp.zeros((), BF16)) * ea
                        g[v] = term if g[v] is None else g[v] + term
                for v in range(n_v):
                    rs = slice(il * nk + v * PACK, il * nk + (v + 1) * PACK)
                    w_new[rs, ls] = g[v] * _gelu(st[rs, ls]).astype(BF16)

    def step(w_new, w_old):
        score(0)
        for k in range(n_sub):
            if k + 1 < n_sub:
                score(k + 1)
            project(k, w_old)
            route(k, w_new)

    @pl.when(s % 2 == 0)
    def _():
        step(wt_a, wt_b)

    @pl.when(s % 2 == 1)
    def _():
        step(wt_b, wt_a)

    @pl.when(jnp.logical_and(e_lag == n_e - 1, s > 0))
    def _():
        y = acc[...].T
        v = alpha * x1_ref[...] + mod_ref[0, 5:6, :] * y
        o_ref[...] = _ln(v) * g_ref[...] + b_ref[...]


def _peer(u2t, u16, vt16, tabs, x1, mod3, mod_row, ln_g, ln_b, heads, nk, tb, eb, alpha):
    d, n = x1.shape[1], u2t.shape[1]
    n_e = 2 * u16.shape[0] // eb
    n_steps = (n // tb) * n_e + 1

    def cur(s):
        c = jnp.minimum(s, n_steps - 2)
        return c // n_e, c % n_e

    def old(s):
        c = jnp.maximum(s - 1, 0)
        return c // n_e, c % n_e

    packed = pl.BlockSpec((heads, nk // 2, tb), lambda s: (0, 0, cur(s)[0]))
    key_rows = pl.BlockSpec((heads, eb // nk, tb), lambda s: (0, cur(s)[1], cur(s)[0]))
    return pl.pallas_call(
        functools.partial(_peer_kernel, heads=heads, nk=nk, alpha=alpha, n_e=n_e),
        out_shape=jax.ShapeDtypeStruct((n, d), F32),
        grid=(n_steps,),
        in_specs=[pl.BlockSpec((d // 2, tb), lambda s: (0, cur(s)[0])),
                  pl.BlockSpec((eb // 2, d), lambda s: (cur(s)[1], 0)),
                  pl.BlockSpec((d // 2, eb), lambda s: (0, old(s)[1])),
                  packed, key_rows, key_rows, packed,
                  _resident((tb, d), lambda s: (old(s)[0], 0)),
                  pl.BlockSpec((1, 6, d), lambda s: (mod_row(old(s)[0]), 0, 0)),
                  pl.BlockSpec((1, d), lambda s: (0, 0)),
                  pl.BlockSpec((1, d), lambda s: (0, 0))],
        out_specs=pl.BlockSpec((tb, d), lambda s: (old(s)[0], 0)),
        scratch_shapes=[pltpu.VMEM((d, tb), F32), pltpu.VMEM((eb, tb), F32),
                        pltpu.VMEM((eb, tb), BF16), pltpu.VMEM((eb, tb), BF16)],
        compiler_params=_cparams(("arbitrary",)),
        name="peer_dense",
    )(u2t, u16, vt16, *tabs, x1, mod3, ln_g, ln_b)


def _grid_pos_table(rows, cols, d):
    quarter = d // 4
    omega = 1.0 / (10000.0 ** (jnp.arange(quarter, dtype=F32) / quarter))
    pr = jnp.arange(rows, dtype=F32)[:, None] * omega
    pc = jnp.arange(cols, dtype=F32)[:, None] * omega
    er = jnp.concatenate([jnp.sin(pr), jnp.cos(pr)], axis=-1)
    ec = jnp.concatenate([jnp.sin(pc), jnp.cos(pc)], axis=-1)
    emb = jnp.concatenate([jnp.broadcast_to(er[:, None, :], (rows, cols, d // 2)),
                           jnp.broadcast_to(ec[None, :, :], (rows, cols, d // 2))], axis=-1)
    return emb.reshape(rows * cols, d)


def _pack_block_diag(w, width):
    nblk, k, _ = w.shape
    per = width // k
    wr = w.reshape(nblk // per, per, k, k)
    eye = jnp.eye(per, dtype=w.dtype)
    return jnp.einsum("spkj,pq->spkqj", wr, eye).reshape(nblk // per, width, width)


def _pack_rows(x16):
    b = lax.bitcast_convert_type(x16, jnp.uint16).astype(jnp.uint32)
    return lax.bitcast_convert_type(b[0::2] | (b[1::2] << 16), jnp.int32)


def _tile(t, pref):
    return pref if t % pref == 0 else t


def kernel(x_prompt, x_sample, state_rglru, c, c_ctx, w_ada, b_ada, w_in, conv_w, conv_b, conv_ln_g, conv_ln_b, lru_conv_w, lru_conv_b, lru_wa, lru_ba, lru_wx, lru_bx, lru_lam, w_out, ln1_g, ln1_b, w_query, sub_keys, peer_u, peer_v, ln2_g, ln2_b):
    depth, d, _ = w_in.shape
    bp, tp, _ = x_prompt.shape
    bs, ts, _ = x_sample.shape
    cw = conv_w.shape[-1]
    heads, _, nk, dk = sub_keys.shape[1:]
    alpha = (2.0 * depth) ** 0.25
    assert lru_conv_w.shape[-1] == cw and w_in.shape[-1] == 4 * cw and nk == dk == LANES

    pos = _grid_pos_table(ts // GRID_W, GRID_W, d)
    n_ctx = 1
    rows = -(-(n_ctx + bs) // SUBLANES) * SUBLANES
    cvecs = jnp.zeros((rows, d), F32).at[0].set(c_ctx).at[n_ctx:n_ctx + bs].set(c)
    sb = min(MXU_DIM, cw)

    x_p, x_s = x_prompt, x_sample
    finals = []
    for l in range(depth):
        mod3 = _modulation(cvecs, w_ada[l], b_ada[l]).reshape(rows, 6, d)
        w_in16 = w_in[l].astype(BF16)
        w_out16 = w_out[l].astype(BF16)
        wq_t16 = w_query[l].T.astype(BF16)
        keys16 = sub_keys[l].reshape(heads * 2, nk, dk).astype(BF16)
        u16 = _pack_rows(peer_u[l].astype(BF16))
        vt16 = _pack_rows(peer_v[l].T.astype(BF16))
        prm = {
            "conv_w": conv_w[l], "conv_b": conv_b[l][None], "conv_ln_g": conv_ln_g[l][None],
            "conv_ln_b": conv_ln_b[l][None], "lru_conv_w": lru_conv_w[l], "lru_conv_b": lru_conv_b[l][None],
            "gate_w": jnp.stack([_pack_block_diag(lru_wa[l, 0], sb), _pack_block_diag(lru_wx[l, 0], sb),
                                 _pack_block_diag(lru_wa[l, 1], sb), _pack_block_diag(lru_wx[l, 1], sb)]
                                ).astype(BF16),
            "gate_b": jnp.stack([lru_ba[l, 0], lru_bx[l, 0], lru_ba[l, 1], lru_bx[l, 1]]),
            "lam": lru_lam[l],
        }

        def run_group(x, pos_tab, mod_row_seq, mod_row_tok, h0, tb):
            b, t, _ = x.shape
            tt = _tile(t, 256)
            p = _inproj(x, pos_tab, mod3, mod_row_seq, w_in16, tt)
            za, hf, ab, bb, hfin_f = _mixer_fwd(p, h0, prm, tt)
            x1, u2t, hfin_b = _mixer_bwd(x, pos_tab, mod3, mod_row_seq, p, za, hf, ab, bb, h0, w_out16,
                                         ln1_g[l][None], ln1_b[l][None], tt, alpha)
            tabs = _route(u2t, wq_t16, keys16, heads, nk, _tile(b * t, 256))
            x2 = _peer(u2t, u16, vt16, tabs, x1.reshape(b * t, d), mod3, mod_row_tok(tb),
                       ln2_g[l][None], ln2_b[l][None], heads, nk, tb, _tile(nk * nk, SUBLANES * nk), alpha)
            return x2.reshape(b, t, d), jnp.concatenate([hfin_f, hfin_b], axis=1)

        tb_p = _tile(bp * tp, 512)
        tb_s = _tile(ts, 512)
        x_p, h_fin = run_group(x_p, None, lambda i: 0, lambda tb: (lambda i: 0),
                               jnp.zeros((bp, 2, cw), F32), tb_p)
        finals.append(h_fin)
        x_s, _ = run_group(x_s, pos if l == 0 else None, lambda i: n_ctx + i,
                           lambda tb: (lambda i: n_ctx + (i * tb) // ts),
                           state_rglru[:, l], tb_s)
    return (x_p, x_s, jnp.stack(finals, axis=1))
```
